```python
import jax, jax.numpy as jnp
from jax import lax
import numpy as np

D_MODEL = 1024
BATCH = 32
SEQ = 2048
DEPTH = 1

D_MIX = D_MODEL
D_HGRN = D_MIX // 2
N_HGRN_HEADS = 4
HGRN_HEAD_DIM = D_HGRN // N_HGRN_HEADS
D_CONV = D_MIX - D_HGRN
N_CONV_GROUPS = 8
CONV_WIDTH = 31
D_FF = 2816
CHUNK = 64
ALPHA = (2.0 * DEPTH) ** 0.25
BETA = (8.0 * DEPTH) ** -0.25
LN_EPS = 1e-5
RMS_EPS = 1e-6
D_IN = 4 * D_HGRN + 2 * D_CONV

kernel_name = 'hgrn2_conformer_conv_hybrid_deepnorm'


def layer_norm(x, g, b):
    xf = x.astype(jnp.float32)
    mu = jnp.mean(xf, axis=-1, keepdims=True)
    var = jnp.mean(jnp.square(xf - mu), axis=-1, keepdims=True)
    y = (xf - mu) * lax.rsqrt(var + LN_EPS) * g.astype(jnp.float32) + b.astype(jnp.float32)
    return y.astype(x.dtype)


def swiglu(x, w_gate, w_up, w_down):
    return (jax.nn.silu(x @ w_gate) * (x @ w_up)) @ w_down


def hgrn2_chunkwise(q, k, v, log_f):
    B, T, H, dk = q.shape
    dv = v.shape[-1]
    n = T // CHUNK

    def to_chunks(a):
        a = a.astype(jnp.float32)
        return a.reshape(B, n, CHUNK, H, a.shape[-1]).transpose(1, 0, 3, 2, 4)

    qc, kc, vc, gc = to_chunks(q), to_chunks(k), to_chunks(v), to_chunks(log_f)
    causal = jnp.tril(jnp.ones((CHUNK, CHUNK), dtype=bool))[:, :, None]

    def step(S, inp):
        qi, ki, vi, gi = inp
        b = jnp.cumsum(gi, axis=-2)
        diff = b[..., :, None, :] - b[..., None, :, :]
        decay = jnp.exp(jnp.where(causal, diff, -jnp.inf))
        scores = jnp.einsum('bhtk,bhsk,bhtsk->bhts', qi, ki, decay)
        o_intra = jnp.einsum('bhts,bhsv->bhtv', scores, vi)
        o_inter = jnp.einsum('bhtk,bhkv->bhtv', qi * jnp.exp(b), S)
        b_last = b[..., -1:, :]
        S_new = S * jnp.exp(b_last[..., 0, :])[..., None] + jnp.einsum(
            'bhsk,bhsv->bhkv', ki * jnp.exp(b_last - b), vi)
        return S_new, o_intra + o_inter

    S0 = jnp.zeros((B, H, dk, dv), jnp.float32)
    _, o = lax.scan(step, S0, (qc, kc, vc, gc))
    return o.transpose(1, 0, 3, 2, 4).reshape(B, T, H * dv)


def setup_inputs(seed: int = 0) -> dict:
    key = jax.random.key(seed)
    ks = jax.random.split(key, 24)
    L = DEPTH

    def normal(k, shape, scale):
        return scale * jax.random.normal(k, shape, jnp.float32)

    return {
        'x': normal(ks[0], (BATCH, SEQ, D_MODEL), 1.0),
        'ffn1_w_gate': normal(ks[1], (L, D_MODEL, D_FF), D_MODEL ** -0.5),
        'ffn1_w_up': normal(ks[2], (L, D_MODEL, D_FF), D_MODEL ** -0.5),
        'ffn1_w_down': normal(ks[3], (L, D_FF, D_MODEL), BETA * D_FF ** -0.5),
        'ln1_g': 1.0 + normal(ks[4], (L, D_MODEL), 0.02),
        'ln1_b': normal(ks[5], (L, D_MODEL), 0.02),
        'w_in': normal(ks[6], (L, D_MODEL, D_IN), D_MODEL ** -0.5),
        'lb_logits': normal(ks[7], (L + 1, D_HGRN), 0.1),
        'hgrn_norm_g': 1.0 + normal(ks[8], (L, D_HGRN), 0.02),
        'conv_w': normal(ks[9], (L, CONV_WIDTH, 1, D_CONV), CONV_WIDTH ** -0.5),
        'conv_b': normal(ks[10], (L, D_CONV), 0.02),
        'conv_ln_g': 1.0 + normal(ks[11], (L, D_CONV), 0.02),
        'conv_ln_b': normal(ks[12], (L, D_CONV), 0.02),
        'w_out': normal(ks[13], (L, D_MIX, D_MODEL), BETA * D_MIX ** -0.5),
        'ln2_g': 1.0 + normal(ks[14], (L, D_MODEL), 0.02),
        'ln2_b': normal(ks[15], (L, D_MODEL), 0.02),
        'ffn2_w_gate': normal(ks[16], (L, D_MODEL, D_FF), D_MODEL ** -0.5),
        'ffn2_w_up': normal(ks[17], (L, D_MODEL, D_FF), D_MODEL ** -0.5),
        'ffn2_w_down': normal(ks[18], (L, D_FF, D_MODEL), BETA * D_FF ** -0.5),
        'ln3_g': 1.0 + normal(ks[19], (L, D_MODEL), 0.02),
        'ln3_b': normal(ks[20], (L, D_MODEL), 0.02),
    }


def reference(x, ffn1_w_gate, ffn1_w_up, ffn1_w_down, ln1_g, ln1_b, w_in, lb_logits,
              hgrn_norm_g, conv_w, conv_b, conv_ln_g, conv_ln_b, w_out, ln2_g, ln2_b,
              ffn2_w_gate, ffn2_w_up, ffn2_w_down, ln3_g, ln3_b):
    B, T, _ = x.shape
    lower_bounds = jnp.cumsum(jax.nn.softmax(lb_logits.astype(jnp.float32), axis=0), axis=0)

    for l in range(DEPTH):
        x = layer_norm(ALPHA * x + 0.5 * swiglu(x, ffn1_w_gate[l], ffn1_w_up[l], ffn1_w_down[l]),
                       ln1_g[l], ln1_b[l])

        proj = x @ w_in[l]
        q, f_raw, i_val, g_out, conv_val, conv_gate = jnp.split(
            proj, [D_HGRN, 2 * D_HGRN, 3 * D_HGRN, 4 * D_HGRN, 4 * D_HGRN + D_CONV], axis=-1)

        lb = lower_bounds[l]
        f = lb + (1.0 - lb) * jax.nn.sigmoid(f_raw.astype(jnp.float32))
        k = 1.0 - f
        log_f = jnp.log(f)

        def heads(a):
            return a.reshape(B, T, N_HGRN_HEADS, HGRN_HEAD_DIM)

        o = hgrn2_chunkwise(heads(q), heads(k), heads(i_val), heads(log_f))
        oh = o.reshape(B, T, N_HGRN_HEADS, HGRN_HEAD_DIM)
        oh = oh * lax.rsqrt(jnp.mean(jnp.square(oh), axis=-1, keepdims=True) + RMS_EPS)
        o_a = (oh.reshape(B, T, D_HGRN) * hgrn_norm_g[l].astype(jnp.float32)
               * jax.nn.sigmoid(g_out.astype(jnp.float32))).astype(x.dtype)

        u = conv_val * jax.nn.sigmoid(conv_gate)
        c = lax.conv_general_dilated(
            u, conv_w[l], window_strides=(1,), padding=[(CONV_WIDTH - 1, 0)],
            dimension_numbers=('NWC', 'WIO', 'NWC'), feature_group_count=D_CONV) + conv_b[l]
        o_b = jax.nn.silu(layer_norm(c, conv_ln_g[l], conv_ln_b[l]))

        mix = jnp.concatenate([o_a, o_b], axis=-1) @ w_out[l]
        x = layer_norm(ALPHA * x + mix, ln2_g[l], ln2_b[l])

        x = layer_norm(ALPHA * x + 0.5 * swiglu(x, ffn2_w_gate[l], ffn2_w_up[l], ffn2_w_down[l]),
                       ln3_g[l], ln3_b[l])
    return x
```

```python
import functools

import jax
import jax.numpy as jnp
from jax import lax
from jax.experimental import pallas as pl
from jax.experimental.pallas import tpu as pltpu

F32 = jnp.float32
BF16 = jnp.bfloat16

DEPTH = 1
ALPHA = (2.0 * DEPTH) ** 0.25
LN_EPS = 1e-5
RMS_EPS = 1e-6
N_HGRN_HEADS = 4
CONV_WIDTH = 31

LANES = 128
SUBLANES = 8
VMEM_BYTES_V7X = 64 * 1024 * 1024

TOKEN_TILE = 512
TIME_TILE = 512
FF_CHUNK = 512
HGRN_CHUNK = 128
HGRN_LEVELS = 7
CONV_HALO = 32
CONV_ROWS = 64


def _layer_norm(y, g, b):
    mu = jnp.mean(y, axis=-1, keepdims=True)
    d = y - mu
    var = jnp.mean(d * d, axis=-1, keepdims=True)
    return d * lax.rsqrt(var + LN_EPS) * g + b


def _dot(a, b):
    return jnp.dot(a, b, preferred_element_type=F32)


def _dot_nt(a, b):
    return lax.dot_general(a, b, (((1,), (1,)), ((), ())), preferred_element_type=F32)


def _dot_tn(a, b):
    return lax.dot_general(a, b, (((0,), (0,)), ((), ())), preferred_element_type=F32)


def _swiglu(xb, wg_ref, wu_ref, wd_ref, a_scr):
    d_ff = wg_ref.shape[1]
    for c0 in range(0, d_ff, FF_CHUNK):
        cw = min(FF_CHUNK, d_ff - c0)
        g = _dot(xb, wg_ref[:, c0:c0 + cw])
        u = _dot(xb, wu_ref[:, c0:c0 + cw])
        a_scr[:, c0:c0 + cw] = (g * jax.nn.sigmoid(g) * u).astype(BF16)
    return _dot(a_scr[...], wd_ref[...])


def _ffn_proj_kernel(x_ref, wg_ref, wu_ref, wd_ref, lng_ref, lnb_ref, win_ref, lbl_ref,
                     x1_ref, q_ref, kk_ref, lf_ref, iv_ref, sg_ref, u_ref, a_scr):
    dh = q_ref.shape[1]
    xf = x_ref[...]
    y = _swiglu(xf.astype(BF16), wg_ref, wu_ref, wd_ref, a_scr)
    x1 = _layer_norm(ALPHA * xf + 0.5 * y, lng_ref[...], lnb_ref[...])
    x1_ref[...] = x1
    x1b = x1.astype(BF16)

    def proj(j):
        return _dot(x1b, win_ref[:, j * dh:(j + 1) * dh])

    q_ref[...] = proj(0).astype(BF16)
    logits = lbl_ref[...]
    e = jnp.exp(logits - jnp.max(logits, axis=0, keepdims=True))
    lb = e[0:1, :] / jnp.sum(e, axis=0, keepdims=True)
    f = lb + (1.0 - lb) * jax.nn.sigmoid(proj(1))
    kk_ref[...] = (1.0 - f).astype(BF16)
    lf_ref[...] = jnp.log(f)
    iv_ref[...] = proj(2).astype(BF16)
    sg_ref[...] = jax.nn.sigmoid(proj(3)).astype(BF16)
    u_ref[...] = (proj(4) * jax.nn.sigmoid(proj(5))).astype(BF16)


def _pair_levels(c):
    t = lax.broadcasted_iota(jnp.int32, (c, c), 0)
    s = lax.broadcasted_iota(jnp.int32, (c, c), 1)
    x = t ^ s
    lv = jnp.zeros((c, c), jnp.int32)
    for l in range(1, HGRN_LEVELS):
        lv = lv + (x >= (1 << l)).astype(jnp.int32)
    return jnp.where(s < t, lv, jnp.where(s == t, HGRN_LEVELS, -1))


def _neg_abs_log_decay(level, b, lf, row):
    c, w = b.shape
    half = 1 << level
    if level == 0:
        return jnp.where((row & 1) == 1, lf, 0.0)
    if level == 1:
        r4 = row & 3
        prev = pltpu.roll(lf, 1, 0)
        nxt = pltpu.roll(lf, c - 1, 0)
        return jnp.where(r4 == 2, lf, jnp.where(r4 == 3, lf + prev, jnp.where(r4 == 0, nxt, 0.0)))
    blk = 2 * half
    b3 = b.reshape(c // blk, blk, w)
    ref = jnp.broadcast_to(b3[:, half - 1:half, :], (c // blk, blk, w)).reshape(c, w)
    return -jnp.abs(b - ref)


def _hgrn_conv_kernel(q_ref, kk_ref, lf_ref, iv_ref, sg_ref, u_ref, ng_ref, cw_ref, cb_ref,
                      clg_ref, clb_ref, oa_ref, ob_ref, st_scr, ext_scr):
    tt, dh = q_ref.shape
    hd = dh // N_HGRN_HEADS
    c = HGRN_CHUNK

    @pl.when(pl.program_id(1) == 0)
    def _():
        st_scr[...] = jnp.zeros_like(st_scr)
        ext_scr[0:CONV_HALO, :] = jnp.zeros((CONV_HALO, ext_scr.shape[1]), F32)

    tri = (lax.broadcasted_iota(jnp.int32, (c, c), 1)
           <= lax.broadcasted_iota(jnp.int32, (c, c), 0)).astype(F32)
    pair_level = _pair_levels(c)
    row = lax.broadcasted_iota(jnp.int32, (c, hd), 0)

    def chunk_body(ci, carry):
        r0 = pl.multiple_of(ci * c, c)
        rows = pl.ds(r0, c)
        lf_all = lf_ref[rows, :]
        b_all = jnp.dot(tri, lf_all, precision=lax.Precision.HIGHEST, preferred_element_type=F32)
        for h in range(N_HGRN_HEADS):
            hs = slice(h * hd, (h + 1) * hd)
            lf = lf_all[:, hs]
            b = b_all[:, hs]
            qb16 = q_ref[rows, hs]
            kb16 = kk_ref[rows, hs]
            v = iv_ref[rows, hs]
            qf = qb16.astype(F32)
            kf = kb16.astype(F32)
            a = jnp.where(pair_level == HGRN_LEVELS, _dot_nt(qb16, kb16), 0.0)
            for level in range(HGRN_LEVELS):
                e = jnp.exp(_neg_abs_log_decay(level, b, lf, row))
                a_l = _dot_nt((qf * e).astype(BF16), (kf * e).astype(BF16))
                a = jnp.where(pair_level == level, a_l, a)
            st = st_scr[h]
            o = _dot(a.astype(BF16), v) + _dot_nt((qf * jnp.exp(b)).astype(BF16), st.astype(BF16))
            b_last = b[c - 1:c, :]
            kd = (kf * jnp.exp(b_last - b)).astype(BF16)
            st_scr[h] = st * jnp.exp(b_last) + _dot_tn(v, kd)
            ms = jnp.mean(o * o, axis=-1, keepdims=True)
            oa = o * lax.rsqrt(ms + RMS_EPS) * ng_ref[:, hs] * sg_ref[rows, hs].astype(F32)
            oa_ref[rows, hs] = oa.astype(BF16)
        return carry

    lax.fori_loop(0, tt // c, chunk_body, 0)

    ext_scr[CONV_HALO:CONV_HALO + tt, :] = u_ref[...].astype(F32)
    first_tap = CONV_HALO - (CONV_WIDTH - 1)

    for r0 in range(0, tt, CONV_ROWS):
        acc = jnp.broadcast_to(cb_ref[...], (CONV_ROWS, dh))
        for j in range(CONV_WIDTH):
            lo = r0 + first_tap + j
            acc = acc + cw_ref[j:j + 1, :] * ext_scr[lo:lo + CONV_ROWS, :]
        z = _layer_norm(acc, clg_ref[...], clb_ref[...])
        ob_ref[r0:r0 + CONV_ROWS, :] = (z * jax.nn.sigmoid(z)).astype(BF16)
    ext_scr[0:CONV_HALO, :] = ext_scr[tt:tt + CONV_HALO, :]


def _out_ffn_kernel(oa_ref, ob_ref, x1_ref, wo_ref, l2g_ref, l2b_ref, wg_ref, wu_ref, wd_ref,
                    l3g_ref, l3b_ref, out_ref, a_scr):
    dh = oa_ref.shape[1]
    mix = _dot(oa_ref[...], wo_ref[0:dh, :]) + _dot(ob_ref[...], wo_ref[dh:, :])
    x2 = _layer_norm(ALPHA * x1_ref[...] + mix, l2g_ref[...], l2b_ref[...])
    y = _swiglu(x2.astype(BF16), wg_ref, wu_ref, wd_ref, a_scr)
    out_ref[...] = _layer_norm(ALPHA * x2 + 0.5 * y, l3g_ref[...], l3b_ref[...])


def _resident(shape):
    return pl.BlockSpec(shape, lambda *_: (0,) * len(shape), pipeline_mode=pl.Buffered(1))


def _vmem_limit(resident_bytes, streamed_bytes, scratch_bytes, temp_bytes):
    need = resident_bytes + 2 * streamed_bytes + scratch_bytes + temp_bytes
    return int(min(need, VMEM_BYTES_V7X - 4 * 1024 * 1024))


def kernel(x, ffn1_w_gate, ffn1_w_up, ffn1_w_down, ln1_g, ln1_b, w_in, lb_logits, hgrn_norm_g,
           conv_w, conv_b, conv_ln_g, conv_ln_b, w_out, ln2_g, ln2_b, ffn2_w_gate, ffn2_w_up,
           ffn2_w_down, ln3_g, ln3_b):
    B, T, D = x.shape
    N = B * T
    d_ff = ffn1_w_gate.shape[-1]
    dh = hgrn_norm_g.shape[-1]
    d_in = w_in.shape[-1]
    tm = min(TOKEN_TILE, N)
    tt = min(TIME_TILE, T)
    assert N % tm == 0 and T % tt == 0 and tt % HGRN_CHUNK == 0 and tt % CONV_ROWS == 0
    assert d_in == 6 * dh and w_out.shape[1] == 2 * dh and dh % N_HGRN_HEADS == 0
    assert (1 << HGRN_LEVELS) == HGRN_CHUNK and CONV_HALO >= CONV_WIDTH - 1

    l = 0
    bf = lambda w: w.astype(BF16)
    row = lambda v: v.reshape(1, -1).astype(F32)
    x2d = x.reshape(N, D)

    tok = lambda w: pl.BlockSpec((tm, w), lambda i: (i, 0))
    ffn_w_bytes = 3 * D * d_ff * 2
    act_temp = tm * (4 * FF_CHUNK * 4 + 4 * D * 4)

    x1, q, kk, lf, iv, sg, u = pl.pallas_call(
        _ffn_proj_kernel,
        grid=(N // tm,),
        in_specs=[tok(D), _resident((D, d_ff)), _resident((D, d_ff)), _resident((d_ff, D)),
                  _resident((1, D)), _resident((1, D)), _resident((D, d_in)),
                  _resident((DEPTH + 1, dh))],
        out_specs=[tok(D)] + [tok(dh)] * 6,
        out_shape=[jax.ShapeDtypeStruct((N, D), F32),
                   jax.ShapeDtypeStruct((N, dh), BF16), jax.ShapeDtypeStruct((N, dh), BF16),
                   jax.ShapeDtypeStruct((N, dh), F32), jax.ShapeDtypeStruct((N, dh), BF16),
                   jax.ShapeDtypeStruct((N, dh), BF16), jax.ShapeDtypeStruct((N, dh), BF16)],
        scratch_shapes=[pltpu.VMEM((tm, d_ff), BF16)],
        compiler_params=pltpu.CompilerParams(
            dimension_semantics=("arbitrary",),
            vmem_limit_bytes=_vmem_limit(ffn_w_bytes + D * d_in * 2,
                                         tm * (2 * D * 4 + 5 * dh * 2 + dh * 4),
                                         tm * d_ff * 2, act_temp)),
        name="ffn1_ln1_proj",
    )(x2d, bf(ffn1_w_gate[l]), bf(ffn1_w_up[l]), bf(ffn1_w_down[l]), row(ln1_g[l]), row(ln1_b[l]),
      bf(w_in[l]), lb_logits.astype(F32))

    seq = lambda a: a.reshape(B, T, dh)
    seq_spec = pl.BlockSpec((None, tt, dh), lambda b, t: (b, t, 0))
    oa, ob = pl.pallas_call(
        _hgrn_conv_kernel,
        grid=(B, T // tt),
        in_specs=[seq_spec] * 6 + [_resident((1, dh)), _resident((CONV_WIDTH, dh)),
                                   _resident((1, dh)), _resident((1, dh)), _resident((1, dh))],
        out_specs=[seq_spec, seq_spec],
        out_shape=[jax.ShapeDtypeStruct((B, T, dh), BF16)] * 2,
        scratch_shapes=[pltpu.VMEM((N_HGRN_HEADS, dh // N_HGRN_HEADS, dh // N_HGRN_HEADS), F32),
                        pltpu.VMEM((tt + CONV_HALO, dh), F32)],
        compiler_params=pltpu.CompilerParams(dimension_semantics=("arbitrary", "arbitrary")),
        name="hgrn2_conv",
    )(seq(q), seq(kk), seq(lf), seq(iv), seq(sg), seq(u), row(hgrn_norm_g[l]),
      conv_w[l].reshape(CONV_WIDTH, dh).astype(F32), row(conv_b[l]), row(conv_ln_g[l]),
      row(conv_ln_b[l]))

    out = pl.pallas_call(
        _out_ffn_kernel,
        grid=(N // tm,),
        in_specs=[tok(dh), tok(dh), tok(D), _resident((2 * dh, D)), _resident((1, D)),
                  _resident((1, D)), _resident((D, d_ff)), _resident((D, d_ff)),
                  _resident((d_ff, D)), _resident((1, D)), _resident((1, D))],
        out_specs=tok(D),
        out_shape=jax.ShapeDtypeStruct((N, D), F32),
        scratch_shapes=[pltpu.VMEM((tm, d_ff), BF16)],
        compiler_params=pltpu.CompilerParams(
            dimension_semantics=("arbitrary",),
            vmem_limit_bytes=_vmem_limit(ffn_w_bytes + 2 * dh * D * 2,
                                         tm * (2 * D * 4 + 2 * dh * 2),
                                         tm * d_ff * 2, act_temp)),
        name="outproj_ln2_ffn2_ln3",
    )(oa.reshape(N, dh), ob.reshape(N, dh), x1, bf(w_out[l]), row(ln2_g[l]), row(ln2_b[l]),
      bf(ffn2_w_gate[l]), bf(ffn2_w_up[l]), bf(ffn2_w_down[l]), row(ln3_g[l]), row(ln3_b[l]))
    return out.reshape(B, T, D)
```

```python
import jax
import jax.numpy as jnp
from jax import lax
from jax.experimental import pallas as pl
from jax.experimental.pallas import tpu as pltpu

F32 = jnp.float32
BF16 = jnp.bfloat16

DEPTH = 1
ALPHA = (2.0 * DEPTH) ** 0.25
LN_EPS = 1e-5
RMS_EPS = 1e-6
LOG2_E = 1.4426950408889634
N_HGRN_HEADS = 4
CONV_WIDTH = 31

LANES = 128
SUBLANES = 8
VMEM_BYTES_V7X = 64 * 1024 * 1024

TOKEN_TILE = 512
TIME_TILE = 512
FF_CHUNK = 512
HGRN_CHUNK = 128
HGRN_LEVELS = 7
CONV_HALO = 32
CONV_ROWS = 64


def _layer_norm(y, g, b):
    mu = jnp.mean(y, axis=-1, keepdims=True)
    d = y - mu
    var = jnp.mean(d * d, axis=-1, keepdims=True)
    return d * lax.rsqrt(var + LN_EPS) * g + b


def _dot(a, b):
    return jnp.dot(a, b, preferred_element_type=F32)


def _dot_nt(a, b):
    return lax.dot_general(a, b, (((1,), (1,)), ((), ())), preferred_element_type=F32)


def _dot_tn(a, b):
    return lax.dot_general(a, b, (((0,), (0,)), ((), ())), preferred_element_type=F32)


def _swiglu(xb, wg_ref, wu_ref, wd_ref, a_scr):
    d_ff = wg_ref.shape[1]
    for c0 in range(0, d_ff, FF_CHUNK):
        cw = min(FF_CHUNK, d_ff - c0)
        g = _dot(xb, wg_ref[:, c0:c0 + cw])
        u = _dot(xb, wu_ref[:, c0:c0 + cw])
        a_scr[:, c0:c0 + cw] = (g * jax.nn.sigmoid(g) * u).astype(BF16)
    return _dot(a_scr[...], wd_ref[...])


def _ffn_proj_kernel(x_ref, wg_ref, wu_ref, wd_ref, lng_ref, lnb_ref, win_ref, lbl_ref,
                     x1_ref, q_ref, kk_ref, lf_ref, lf3_ref, iv_ref, sg_ref, u_ref, a_scr):
    dh = q_ref.shape[1]
    xf = x_ref[...]
    y = _swiglu(xf.astype(BF16), wg_ref, wu_ref, wd_ref, a_scr)
    x1 = _layer_norm(ALPHA * xf + 0.5 * y, lng_ref[...], lnb_ref[...])
    x1_ref[...] = x1
    x1b = x1.astype(BF16)

    def proj(j):
        return _dot(x1b, win_ref[:, j * dh:(j + 1) * dh])

    q_ref[...] = proj(0).astype(BF16)
    logits = lbl_ref[...]
    e = jnp.exp(logits - jnp.max(logits, axis=0, keepdims=True))
    lb = e[0:1, :] / jnp.sum(e, axis=0, keepdims=True)
    f = lb + (1.0 - lb) * jax.nn.sigmoid(proj(1))
    kk_ref[...] = (1.0 - f).astype(BF16)
    lf = jnp.log(f) * LOG2_E
    lf_ref[...] = lf
    hi = lf.astype(BF16)
    r1 = lf - hi.astype(F32)
    mid = r1.astype(BF16)
    lo = (r1 - mid.astype(F32)).astype(BF16)
    c = HGRN_CHUNK
    for ci in range(lf.shape[0] // c):
        for part, term in enumerate((hi, mid, lo)):
            lf3_ref[ci, part * c:(part + 1) * c, :] = term[ci * c:(ci + 1) * c, :]
    iv_ref[...] = proj(2).astype(BF16)
    sg_ref[...] = jax.nn.sigmoid(proj(3)).astype(BF16)
    u_ref[...] = (proj(4) * jax.nn.sigmoid(proj(5))).astype(BF16)


def _pair_levels(c):
    t = lax.broadcasted_iota(jnp.int32, (c, c), 0)
    s = lax.broadcasted_iota(jnp.int32, (c, c), 1)
    x = t ^ s
    lv = jnp.zeros((c, c), jnp.int32)
    for l in range(1, HGRN_LEVELS):
        lv = lv + (x >= (1 << l)).astype(jnp.int32)
    return jnp.where(s < t, lv, jnp.where(s == t, HGRN_LEVELS, -1))


def _neg_abs_log_decay(level, b, lf, row):
    c, w = b.shape
    half = 1 << level
    if level == 0:
        return jnp.where((row & 1) == 1, lf, 0.0)
    if level == 1:
        r4 = row & 3
        prev = pltpu.roll(lf, 1, 0)
        nxt = pltpu.roll(lf, c - 1, 0)
        return jnp.where(r4 == 2, lf, jnp.where(r4 == 3, lf + prev, jnp.where(r4 == 0, nxt, 0.0)))
    blk = 2 * half
    b3 = b.reshape(c // blk, blk, w)
    ref = jnp.broadcast_to(b3[:, half - 1:half, :], (c // blk, blk, w)).reshape(c, w)
    return -jnp.abs(b - ref)


def _hgrn_conv_kernel(q_ref, kk_ref, lf_ref, lf3_ref, iv_ref, sg_ref, u_ref, ng_ref, cw_ref, cb_ref,
                      clg_ref, clb_ref, oa_ref, ob_ref, st_scr, ext_scr, conv_scr):
    tt, dh = q_ref.shape
    hd = dh // N_HGRN_HEADS
    c = HGRN_CHUNK
    n_cb = dh // LANES

    @pl.when(pl.program_id(1) == 0)
    def _():
        st_scr[...] = jnp.zeros_like(st_scr)
        ext_scr[:, 0:CONV_HALO, :] = jnp.zeros((n_cb, CONV_HALO, LANES), F32)

    tri3 = (lax.broadcasted_iota(jnp.int32, (c, 3 * c), 1) % c
            <= lax.broadcasted_iota(jnp.int32, (c, 3 * c), 0)).astype(BF16)
    pair_level = _pair_levels(c)
    row = lax.broadcasted_iota(jnp.int32, (c, hd), 0)

    def chunk_body(ci, carry):
        r0 = pl.multiple_of(ci * c, c)
        rows = pl.ds(r0, c)
        lf_all = lf_ref[rows, :]
        b_all = _dot(tri3, lf3_ref[ci])
        for h in range(N_HGRN_HEADS):
            hs = slice(h * hd, (h + 1) * hd)
            lf = lf_all[:, hs]
            b = b_all[:, hs]
            q16 = q_ref[rows, hs]
            k16 = kk_ref[rows, hs]
            q = q16.astype(F32)
            k = k16.astype(F32)
            v = iv_ref[rows, hs]
            a = jnp.where(pair_level == HGRN_LEVELS, _dot_nt(q16, k16), 0.0)
            for level in range(HGRN_LEVELS):
                e = jnp.exp2(_neg_abs_log_decay(level, b, lf, row))
                a_l = _dot_nt((q * e).astype(BF16), (k * e).astype(BF16))
                a = jnp.where(pair_level == level, a_l, a)
            st = st_scr[h]
            o = _dot(a.astype(BF16), v) + _dot_nt((q * jnp.exp2(b)).astype(BF16), st.astype(BF16))
            b_last = b[c - 1:c, :]
            kd = (k * jnp.exp2(b_last - b)).astype(BF16)
            st_scr[h] = st * jnp.exp2(b_last) + _dot_tn(v, kd)
            ms = jnp.mean(o * o, axis=-1, keepdims=True)
            oa = o * lax.rsqrt(ms + RMS_EPS) * ng_ref[:, hs] * sg_ref[rows, hs].astype(F32)
            oa_ref[rows, hs] = oa.astype(BF16)
        return carry

    lax.fori_loop(0, tt // c, chunk_body, 0, unroll=True)

    for cb in range(n_cb):
        ext_scr[cb, CONV_HALO:CONV_HALO + tt, :] = u_ref[:, cb * LANES:(cb + 1) * LANES].astype(F32)
    first_tap = CONV_HALO - (CONV_WIDTH - 1)
    for cb in range(n_cb):
        cs = slice(cb * LANES, (cb + 1) * LANES)
        for r0 in range(0, tt, CONV_ROWS):
            acc = jnp.broadcast_to(cb_ref[:, cs], (CONV_ROWS, LANES))
            for j in range(CONV_WIDTH):
                lo = r0 + first_tap + j
                acc = acc + cw_ref[j:j + 1, cs] * ext_scr[cb, lo:lo + CONV_ROWS, :]
            conv_scr[r0:r0 + CONV_ROWS, cs] = acc
        ext_scr[cb, 0:CONV_HALO, :] = ext_scr[cb, tt:tt + CONV_HALO, :]
    for r0 in range(0, tt, CONV_ROWS):
        z = _layer_norm(conv_scr[r0:r0 + CONV_ROWS, :], clg_ref[...], clb_ref[...])
        ob_ref[r0:r0 + CONV_ROWS, :] = (z * jax.nn.sigmoid(z)).astype(BF16)


def _out_ffn_kernel(oa_ref, ob_ref, x1_ref, wo_ref, l2g_ref, l2b_ref, wg_ref, wu_ref, wd_ref,
                    l3g_ref, l3b_ref, out_ref, a_scr):
    dh = oa_ref.shape[1]
    mix = _dot(oa_ref[...], wo_ref[0:dh, :]) + _dot(ob_ref[...], wo_ref[dh:, :])
    x2 = _layer_norm(ALPHA * x1_ref[...] + mix, l2g_ref[...], l2b_ref[...])
    y = _swiglu(x2.astype(BF16), wg_ref, wu_ref, wd_ref, a_scr)
    out_ref[...] = _layer_norm(ALPHA * x2 + 0.5 * y, l3g_ref[...], l3b_ref[...])


def _resident(shape):
    return pl.BlockSpec(shape, lambda *_: (0,) * len(shape), pipeline_mode=pl.Buffered(1))


def _vmem_limit(resident_bytes, streamed_bytes, scratch_bytes, temp_bytes):
    need = resident_bytes + 2 * streamed_bytes + scratch_bytes + temp_bytes
    return int(min(need, VMEM_BYTES_V7X - 4 * 1024 * 1024))


def kernel(x, ffn1_w_gate, ffn1_w_up, ffn1_w_down, ln1_g, ln1_b, w_in, lb_logits, hgrn_norm_g,
           conv_w, conv_b, conv_ln_g, conv_ln_b, w_out, ln2_g, ln2_b, ffn2_w_gate, ffn2_w_up,
           ffn2_w_down, ln3_g, ln3_b):
    B, T, D = x.shape
    N = B * T
    d_ff = ffn1_w_gate.shape[-1]
    dh = hgrn_norm_g.shape[-1]
    d_in = w_in.shape[-1]
    tm = min(TOKEN_TILE, N)
    tt = min(TIME_TILE, T)
    assert N % tm == 0 and T % tt == 0 and tt % HGRN_CHUNK == 0 and tt % CONV_ROWS == 0
    assert tm % HGRN_CHUNK == 0 and T % tm == 0
    assert d_in == 6 * dh and w_out.shape[1] == 2 * dh and dh % N_HGRN_HEADS == 0
    assert (1 << HGRN_LEVELS) == HGRN_CHUNK and CONV_HALO >= CONV_WIDTH - 1

    l = 0
    bf = lambda w: w.astype(BF16)
    row = lambda v: v.reshape(1, -1).astype(F32)
    x2d = x.reshape(N, D)

    tok = lambda w: pl.BlockSpec((tm, w), lambda i: (i, 0))
    ffn_w_bytes = 3 * D * d_ff * 2
    act_temp = tm * (4 * FF_CHUNK * 4 + 4 * D * 4)

    chunk3 = pl.BlockSpec((tm // HGRN_CHUNK, 3 * HGRN_CHUNK, dh), lambda i: (i, 0, 0))
    x1, q, kk, lf, lf3, iv, sg, u = pl.pallas_call(
        _ffn_proj_kernel,
        grid=(N // tm,),
        in_specs=[tok(D), _resident((D, d_ff)), _resident((D, d_ff)), _resident((d_ff, D)),
                  _resident((1, D)), _resident((1, D)), _resident((D, d_in)),
                  _resident((DEPTH + 1, dh))],
        out_specs=[tok(D), tok(dh), tok(dh), tok(dh), chunk3, tok(dh), tok(dh), tok(dh)],
        out_shape=[jax.ShapeDtypeStruct((N, D), F32),
                   jax.ShapeDtypeStruct((N, dh), BF16), jax.ShapeDtypeStruct((N, dh), BF16),
                   jax.ShapeDtypeStruct((N, dh), F32),
                   jax.ShapeDtypeStruct((N // HGRN_CHUNK, 3 * HGRN_CHUNK, dh), BF16),
                   jax.ShapeDtypeStruct((N, dh), BF16),
                   jax.ShapeDtypeStruct((N, dh), BF16), jax.ShapeDtypeStruct((N, dh), BF16)],
        scratch_shapes=[pltpu.VMEM((tm, d_ff), BF16)],
        compiler_params=pltpu.CompilerParams(
            dimension_semantics=("arbitrary",),
            vmem_limit_bytes=_vmem_limit(ffn_w_bytes + D * d_in * 2,
                                         tm * (2 * D * 4 + 8 * dh * 2 + dh * 4),
                                         tm * d_ff * 2, act_temp)),
        name="ffn1_ln1_proj",
    )(x2d, bf(ffn1_w_gate[l]), bf(ffn1_w_up[l]), bf(ffn1_w_down[l]), row(ln1_g[l]), row(ln1_b[l]),
      bf(w_in[l]), lb_logits.astype(F32))

    seq = lambda a: a.reshape(B, T, dh)
    seq_spec = pl.BlockSpec((None, tt, dh), lambda b, t: (b, t, 0))
    n_tc = tt // HGRN_CHUNK
    lf3_spec = pl.BlockSpec((n_tc, 3 * HGRN_CHUNK, dh), lambda b, t: (b * (T // tt) + t, 0, 0))
    oa, ob = pl.pallas_call(
        _hgrn_conv_kernel,
        grid=(B, T // tt),
        in_specs=[seq_spec] * 3 + [lf3_spec] + [seq_spec] * 3 + [_resident((1, dh)), _resident((CONV_WIDTH, dh)),
                                   _resident((1, dh)), _resident((1, dh)), _resident((1, dh))],
        out_specs=[seq_spec, seq_spec],
        out_shape=[jax.ShapeDtypeStruct((B, T, dh), BF16)] * 2,
        scratch_shapes=[pltpu.VMEM((N_HGRN_HEADS, dh // N_HGRN_HEADS, dh // N_HGRN_HEADS), F32),
                        pltpu.VMEM((dh // LANES, tt + CONV_HALO, LANES), F32),
                        pltpu.VMEM((tt, dh), F32)],
        compiler_params=pltpu.CompilerParams(dimension_semantics=("arbitrary", "arbitrary")),
        name="hgrn2_conv",
    )(seq(q), seq(kk), seq(lf), lf3, seq(iv), seq(sg), seq(u), row(hgrn_norm_g[l]),
      conv_w[l].reshape(CONV_WIDTH, dh).astype(F32), row(conv_b[l]), row(conv_ln_g[l]),
      row(conv_ln_b[l]))

    out = pl.pallas_call(
        _out_ffn_kernel,
        grid=(N // tm,),
        in_specs=[tok(dh), tok(dh), tok(D), _resident((2 * dh, D)), _resident((1, D)),
                  _resident((1, D)), _resident((D, d_ff)), _resident((D, d_ff)),
                  _resident((d_ff, D)), _resident((1, D)), _resident((1, D))],
        out_specs=tok(D),
        out_shape=jax.ShapeDtypeStruct((N, D), F32),
        scratch_shapes=[pltpu.VMEM((tm, d_ff), BF16)],
        compiler_params=pltpu.CompilerParams(
            dimension_semantics=("arbitrary",),
            vmem_limit_bytes=_vmem_limit(ffn_w_bytes + 2 * dh * D * 2,
                                         tm * (2 * D * 4 + 2 * dh * 2),
                                         tm * d_ff * 2, act_temp)),
        name="outproj_ln2_ffn2_ln3",
    )(oa.reshape(N, dh), ob.reshape(N, dh), x1, bf(w_out[l]), row(ln2_g[l]), row(ln2_b[l]),
      bf(ffn2_w_gate[l]), bf(ffn2_w_up[l]), bf(ffn2_w_down[l]), row(ln3_g[l]), row(ln3_b[l]))
    return out.reshape(B, T, D)
```
